```python
import jax, jax.numpy as jnp
from jax import lax
import numpy as np

D_MODEL = 1024
BATCH = 2
SEQ = 8192
DEPTH = 4
DEC_BATCH = 32
DEC_SEQ = 1
PAST_LEN = 8192
PAGE_SIZE = 128

N_MIXERS = 3
EXPAND = 2
E_WIDTH = EXPAND * D_MODEL
CHUNK = 128
A_GROUPS = 8
CONV_W = 3
C_HEAD_DIM = 128
C_HEADS = E_WIDTH // C_HEAD_DIM
DIL_GROUPS = ((128, 1), (512, 4), (2048, 16))
N_DIL = len(DIL_GROUPS)
ROT_DIM = C_HEAD_DIM // 4
ROPE_THETA = 500000.0
EPS = 1e-6
N_A = (DEPTH + 2) // 3
N_B = (DEPTH + 1) // 3
N_C = DEPTH // 3

kernel_name = 'hybrid_chunkmlp_shortconv_dilatedswa_step'


def _rmsnorm(x, w):
    xf = x.astype(jnp.float32)
    y = xf * lax.rsqrt(jnp.mean(xf * xf, axis=-1, keepdims=True) + EPS)
    return (y * w.astype(jnp.float32)).astype(x.dtype)


def _layernorm(x, g, b):
    xf = x.astype(jnp.float32)
    xc = xf - jnp.mean(xf, axis=-1, keepdims=True)
    y = xc * lax.rsqrt(jnp.mean(xc * xc, axis=-1, keepdims=True) + EPS)
    return (y * g.astype(jnp.float32) + b.astype(jnp.float32)).astype(x.dtype)


def _chunk_mix(v, w_s, b_s):
    bsz, L, E = v.shape
    c = min(L, CHUNK)
    vc = v.reshape(bsz, L // c, c, A_GROUPS, E // A_GROUPS)
    w = jnp.tril(w_s[:, :c, :c])
    out = jnp.einsum('gij,bnjge->bnige', w, vc) + b_s[:, :c].T[None, None, :, :, None]
    return out.reshape(bsz, L, E)


def _mixer_a(h, w_in, ln_g, ln_b, w_s, b_s, w_out):
    z = h @ w_in
    u, v, g = jnp.split(z, 3, axis=-1)
    v = _layernorm(v, ln_g, ln_b)
    y = (u * _chunk_mix(v, w_s, b_s) * jax.nn.silu(g)) @ w_out
    return y, v


def _mixer_b(h, prev, w_in, conv_w, w_out):
    L = h.shape[1]
    z = h @ w_in
    bg, cg, xv, g = jnp.split(z, 4, axis=-1)
    uc = jnp.concatenate([prev, cg * xv], axis=1)
    conv = sum(conv_w[k] * uc[:, k:k + L] for k in range(CONV_W))
    y = (bg * conv * jax.nn.silu(g)) @ w_out
    return y, uc[:, -(CONV_W - 1):]


def _rope_partial(x, pos):
    half = ROT_DIM // 2
    inv = ROPE_THETA ** (-jnp.arange(half, dtype=jnp.float32) * (2.0 / ROT_DIM))
    ang = pos.astype(jnp.float32)[:, None] * inv[None, :]
    shp = (1, pos.shape[0]) + (1,) * (x.ndim - 3) + (half,)
    cos, sin = jnp.cos(ang).reshape(shp), jnp.sin(ang).reshape(shp)
    x1 = x[..., :half].astype(jnp.float32)
    x2 = x[..., half:ROT_DIM].astype(jnp.float32)
    rot = jnp.concatenate([x1 * cos - x2 * sin, x2 * cos + x1 * sin], axis=-1).astype(x.dtype)
    return jnp.concatenate([rot, x[..., ROT_DIM:]], axis=-1)


def _softmax_lse(s):
    m = jnp.max(s, axis=-1, keepdims=True)
    p = jnp.exp(s - m)
    den = jnp.sum(p, axis=-1, keepdims=True)
    return p / den, (m + jnp.log(den))[..., 0]


def _dilated_prompt(q, k, v, dil, win):
    B, S, H, hd = q.shape
    blk = win // dil
    span = dil * blk
    sp = -(-S // span) * span
    nb = sp // span

    def split(t):
        t = jnp.pad(t, ((0, 0), (0, sp - S), (0, 0), (0, 0))).reshape(B, sp // dil, dil, H, hd)
        return jnp.moveaxis(t, 2, 1).reshape(B, dil, nb, blk, H, hd)

    def with_prev(t):
        prev = jnp.pad(t[:, :, :-1], ((0, 0), (0, 0), (1, 0), (0, 0), (0, 0), (0, 0)))
        return jnp.concatenate([prev, t], axis=3)

    qb = split(q)
    kk, vv = with_prev(split(k)), with_prev(split(v))
    s = jnp.einsum('brnqhe,brnkhe->brnhqk', qb, kk).astype(jnp.float32) * (hd ** -0.5)
    ki = jnp.arange(2 * blk)[None, :]
    rel = (jnp.arange(blk)[:, None] + blk) - ki
    band = (rel >= 0) & (rel <= blk)
    first = (jnp.arange(nb)[:, None, None] > 0) | (ki >= blk)[None]
    mask = band[None] & first
    s = jnp.where(mask[:, None], s, -jnp.inf)
    p, lse = _softmax_lse(s)
    o = jnp.einsum('brnhqk,brnkhe->brnqhe', p.astype(vv.dtype), vv).astype(jnp.float32)

    def merge(t):
        t = t.reshape((B, dil, sp // dil) + t.shape[4:])
        return jnp.moveaxis(t, 1, 2).reshape((B, sp) + t.shape[3:])[:, :S]

    return merge(o), merge(jnp.moveaxis(lse, 3, 4))


def _dilated_sample(q, k_new, v_new, kv_buf, dil, win):
    Lb, T, hd = kv_buf.shape[1], q.shape[1], q.shape[-1]
    kc = jnp.concatenate([kv_buf[:, :, 0], k_new], axis=1)
    vc = jnp.concatenate([kv_buf[:, :, 1], v_new], axis=1)
    idx = Lb + jnp.arange(T)[:, None] - dil * jnp.arange(win // dil + 1)[None, :]
    valid = idx >= 0
    idx = jnp.maximum(idx, 0)
    kg, vg = kc[:, idx], vc[:, idx]
    s = jnp.einsum('bthe,btjhe->bthj', q, kg).astype(jnp.float32) * (hd ** -0.5)
    s = jnp.where(valid[None, :, None, :], s, -jnp.inf)
    p, lse = _softmax_lse(s)
    o = jnp.einsum('bthj,btjhe->bthe', p.astype(vg.dtype), vg).astype(jnp.float32)
    return o, lse


def _c_project(h, w_in, pos):
    bsz, L, _ = h.shape
    z = h @ w_in
    qkv = z[..., :3 * N_DIL * E_WIDTH].reshape(bsz, L, 3, N_DIL, C_HEADS, C_HEAD_DIM)
    q = _rope_partial(qkv[:, :, 0], pos)
    k = _rope_partial(qkv[:, :, 1], pos)
    return q, k, qkv[:, :, 2], z[..., 3 * N_DIL * E_WIDTH:]


def _c_finish(outs, lses, gate, w_out):
    wts = jax.nn.softmax(jnp.stack(lses, 0), axis=0)
    o = jnp.einsum('gblh,gblhe->blhe', wts, jnp.stack(outs, 0))
    bsz, L = gate.shape[:2]
    return (o.reshape(bsz, L, E_WIDTH).astype(gate.dtype) * jax.nn.silu(gate)) @ w_out


def setup_inputs(seed: int = 0) -> dict:
    key = jax.random.key(seed)
    ks = jax.random.split(key, 24)

    def nrm(k, shape, scale):
        return jax.random.normal(k, shape, jnp.float32) * scale

    E = E_WIDTH
    inp = {}
    inp['x_prompt'] = nrm(ks[0], (BATCH, SEQ, D_MODEL), 1.0)
    inp['x_sample'] = nrm(ks[1], (DEC_BATCH, DEC_SEQ, D_MODEL), 1.0)
    inp['state_conv'] = nrm(ks[2], (N_B, DEC_BATCH, CONV_W - 1, E), 1.0)
    for gi, (win, _) in enumerate(DIL_GROUPS):
        inp['cache_kv_w%d' % win] = nrm(ks[3 + gi], (N_C, DEC_BATCH, min(win, PAST_LEN), 2, C_HEADS, C_HEAD_DIM), 1.0)
    inp['norm_w'] = 1.0 + nrm(ks[6], (DEPTH, D_MODEL), 0.02)
    inp['final_norm_w'] = 1.0 + nrm(ks[7], (D_MODEL,), 0.02)
    inp['a_w_in'] = nrm(ks[8], (N_A, D_MODEL, 3 * E), D_MODEL ** -0.5)
    inp['a_ln_g'] = 1.0 + nrm(ks[9], (N_A, E), 0.02)
    inp['a_ln_b'] = nrm(ks[10], (N_A, E), 0.02)
    inp['a_w_s'] = nrm(ks[11], (N_A, A_GROUPS, CHUNK, CHUNK), CHUNK ** -0.5)
    inp['a_b_s'] = 1.0 + nrm(ks[12], (N_A, A_GROUPS, CHUNK), 0.1)
    inp['a_w_out'] = nrm(ks[13], (N_A, E, D_MODEL), E ** -0.5)
    inp['b_w_in'] = nrm(ks[14], (N_B, D_MODEL, 4 * E), D_MODEL ** -0.5)
    inp['b_conv_w'] = nrm(ks[15], (N_B, CONV_W, E), CONV_W ** -0.5)
    inp['b_w_out'] = nrm(ks[16], (N_B, E, D_MODEL), E ** -0.5)
    inp['c_w_in'] = nrm(ks[17], (N_C, D_MODEL, (3 * N_DIL + 1) * E), D_MODEL ** -0.5)
    inp['c_w_out'] = nrm(ks[18], (N_C, E, D_MODEL), E ** -0.5)
    return inp


def reference(x_prompt, x_sample, state_conv, cache_kv_w128, cache_kv_w512, cache_kv_w2048,
              norm_w, final_norm_w, a_w_in, a_ln_g, a_ln_b, a_w_s, a_b_s, a_w_out,
              b_w_in, b_conv_w, b_w_out, c_w_in, c_w_out):
    caches = (cache_kv_w128, cache_kv_w512, cache_kv_w2048)
    bsz, S, _ = x_prompt.shape
    T = x_sample.shape[1]
    pos_p = jnp.arange(S)
    pos_s = PAST_LEN + jnp.arange(T)
    hp, hs = x_prompt, x_sample
    v_rows, conv_p, conv_s = [], [], []
    kv_p = [[] for _ in DIL_GROUPS]
    kv_s = [[] for _ in DIL_GROUPS]
    for i in range(DEPTH):
        kind, j = i % N_MIXERS, i // N_MIXERS
        n_p, n_s = _rmsnorm(hp, norm_w[i]), _rmsnorm(hs, norm_w[i])
        if kind == 0:
            yp, _ = _mixer_a(n_p, a_w_in[j], a_ln_g[j], a_ln_b[j], a_w_s[j], a_b_s[j], a_w_out[j])
            ys, v_new = _mixer_a(n_s, a_w_in[j], a_ln_g[j], a_ln_b[j], a_w_s[j], a_b_s[j], a_w_out[j])
            v_rows.append(v_new)
        elif kind == 1:
            zeros = jnp.zeros((bsz, CONV_W - 1, E_WIDTH), n_p.dtype)
            yp, cp = _mixer_b(n_p, zeros, b_w_in[j], b_conv_w[j], b_w_out[j])
            ys, cs = _mixer_b(n_s, state_conv[j], b_w_in[j], b_conv_w[j], b_w_out[j])
            conv_p.append(cp)
            conv_s.append(cs)
        else:
            qp, kp, vp, gp = _c_project(n_p, c_w_in[j], pos_p)
            qs, ks_, vs_, gs = _c_project(n_s, c_w_in[j], pos_s)
            op, lp, osm, lsm = [], [], [], []
            for g, (win, dil) in enumerate(DIL_GROUPS):
                o, l = _dilated_prompt(qp[:, :, g], kp[:, :, g], vp[:, :, g], dil, win)
                op.append(o)
                lp.append(l)
                keep = min(win, S)
                kv_p[g].append(jnp.stack([kp[:, S - keep:, g], vp[:, S - keep:, g]], axis=2))
                o, l = _dilated_sample(qs[:, :, g], ks_[:, :, g], vs_[:, :, g], caches[g][j], dil, win)
                osm.append(o)
                lsm.append(l)
                kv_s[g].append(jnp.stack([ks_[:, :, g], vs_[:, :, g]], axis=2))
            yp = _c_finish(op, lp, gp, c_w_out[j])
            ys = _c_finish(osm, lsm, gs, c_w_out[j])
        hp = hp + yp
        hs = hs + ys
    y_prompt = _rmsnorm(hp, final_norm_w)
    y_sample = _rmsnorm(hs, final_norm_w)
    state_v_chunk_sample = jnp.stack(v_rows)
    state_conv_prompt = jnp.stack(conv_p)
    state_conv_sample = jnp.stack(conv_s)
    kvp = [jnp.stack(l) for l in kv_p]
    kvs = [jnp.stack(l) for l in kv_s]
    return (y_prompt, y_sample, state_v_chunk_sample, state_conv_prompt, state_conv_sample,
            kvp[0], kvs[0], kvp[1], kvs[1], kvp[2], kvs[2])
```

```python
import functools

import jax
import jax.numpy as jnp
from jax import lax
from jax.experimental import pallas as pl
from jax.experimental.pallas import tpu as pltpu

F32 = jnp.float32
BF16 = jnp.bfloat16

N_MIXERS = 3
CHUNK = 128
A_GROUPS = 8
CONV_W = 3
HEAD_DIM = 128
DIL_GROUPS = ((128, 1), (512, 4), (2048, 16))
N_DIL = len(DIL_GROUPS)
ROT_DIM = HEAD_DIM // 4
ROPE_THETA = 500000.0
PAST_LEN = 8192
EPS = 1e-6
ATTN_BLK = 128

VMEM_LIMIT_BYTES = 56 * 1024 * 1024
LANES = 128

TM_LAYER = 256
TM_PROJ = 512


def _cparams(n_axes):
    return pltpu.CompilerParams(
        dimension_semantics=("arbitrary",) * n_axes,
        vmem_limit_bytes=VMEM_LIMIT_BYTES,
    )


def _resident(shape):
    nd = len(shape)
    return pl.BlockSpec(shape, lambda *_: (0,) * nd, pipeline_mode=pl.Buffered(1))


def _rms(x, w):
    return x * lax.rsqrt(jnp.mean(x * x, axis=-1, keepdims=True) + EPS) * w


def _layernorm(v, g, b):
    vc = v - jnp.mean(v, axis=-1, keepdims=True)
    return vc * lax.rsqrt(jnp.mean(vc * vc, axis=-1, keepdims=True) + EPS) * g + b


def _silu(g):
    return g * (1.0 / (1.0 + jnp.exp(-g)))


def _mm(a, w):
    return jnp.dot(a.astype(BF16), w, preferred_element_type=F32)


def _rope_head(zh, cos, s1, s2):
    up = pltpu.roll(zh, HEAD_DIM - ROT_DIM // 2, axis=1)
    dn = pltpu.roll(zh, ROT_DIM // 2, axis=1)
    return zh * cos + up * s1 + dn * s2


def _layer_a_kernel(h_ref, nw_ref, win_ref, lng_ref, lnb_ref, ws_ref, bias_ref, wout_ref,
                    fnw_ref, o_ref, mix_ref, *, final_norm):
    e = lng_ref.shape[-1]
    ge = e // A_GROUPS
    x = h_ref[...]
    tm = x.shape[0]
    z = _mm(_rms(x, nw_ref[...]), win_ref[...])
    u = z[:, :e]
    g = z[:, 2 * e:]
    vb = _layernorm(z[:, e:2 * e], lng_ref[...], lnb_ref[...]).astype(BF16)
    row = lax.broadcasted_iota(jnp.int32, (CHUNK, CHUNK), 0)
    col = lax.broadcasted_iota(jnp.int32, (CHUNK, CHUNK), 1)
    for grp in range(A_GROUPS):
        w = jnp.where(row >= col, ws_ref[grp], jnp.zeros((), BF16))
        cs = slice(grp * ge, (grp + 1) * ge)
        for c in range(tm // CHUNK):
            rs = slice(c * CHUNK, (c + 1) * CHUNK)
            mix_ref[rs, cs] = (
                jnp.dot(w, vb[rs, cs], preferred_element_type=F32) + bias_ref[:, cs])
    a = u * mix_ref[...] * _silu(g)
    out = x + _mm(a, wout_ref[...])
    if final_norm:
        out = _rms(out, fnw_ref[...])
    o_ref[...] = out


def _layer_a(h, nw, win, lng, lnb, ws, bias, wout, fnw, *, final_norm):
    m, d = h.shape
    e = lng.shape[-1]
    tm = TM_LAYER
    return pl.pallas_call(
        functools.partial(_layer_a_kernel, final_norm=final_norm),
        grid=(m // tm,),
        in_specs=[
            pl.BlockSpec((tm, d), lambda i: (i, 0)),
            _resident(nw.shape), _resident(win.shape), _resident(lng.shape),
            _resident(lnb.shape), _resident(ws.shape), _resident(bias.shape),
            _resident(wout.shape), _resident(fnw.shape),
        ],
        out_specs=pl.BlockSpec((tm, d), lambda i: (i, 0)),
        out_shape=jax.ShapeDtypeStruct((m, d), F32),
        scratch_shapes=[pltpu.VMEM((tm, e), F32)],
        compiler_params=_cparams(1),
        name="prompt_layer_a",
    )(h, nw, win, lng, lnb, ws, bias, wout, fnw)


def _layer_b_kernel(h_ref, nw_ref, win_ref, cw_ref, wout_ref, o_ref, st_ref, ubuf_ref, *,
                    tiles_per_seq):
    e = cw_ref.shape[-1]
    halo = 8
    i = pl.program_id(0)

    @pl.when(i % tiles_per_seq == 0)
    def _():
        ubuf_ref[0:halo, :] = jnp.zeros((halo, e), F32)

    x = h_ref[...]
    tm = x.shape[0]
    z = _mm(_rms(x, nw_ref[...]), win_ref[...])
    bg = z[:, :e]
    u = z[:, e:2 * e] * z[:, 2 * e:3 * e]
    g = z[:, 3 * e:]
    ubuf_ref[halo:halo + tm, :] = u
    conv = (cw_ref[0:1, :] * ubuf_ref[halo - 2:halo - 2 + tm, :]
            + cw_ref[1:2, :] * ubuf_ref[halo - 1:halo - 1 + tm, :]
            + cw_ref[2:3, :] * u)
    a = bg * conv * _silu(g)
    o_ref[...] = x + _mm(a, wout_ref[...])
    st_ref[...] = ubuf_ref[halo + tm - (CONV_W - 1):halo + tm, :]
    ubuf_ref[0:halo, :] = ubuf_ref[tm:tm + halo, :]


def _layer_b(h, nw, win, cw, wout, *, batch):
    m, d = h.shape
    e = cw.shape[-1]
    tm = TM_LAYER
    tiles_per_seq = m // batch // tm
    return pl.pallas_call(
        functools.partial(_layer_b_kernel, tiles_per_seq=tiles_per_seq),
        grid=(m // tm,),
        in_specs=[
            pl.BlockSpec((tm, d), lambda i: (i, 0)),
            _resident(nw.shape), _resident(win.shape), _resident(cw.shape),
            _resident(wout.shape),
        ],
        out_specs=[
            pl.BlockSpec((tm, d), lambda i: (i, 0)),
            pl.BlockSpec((None, CONV_W - 1, e), lambda i: (i // tiles_per_seq, 0, 0)),
        ],
        out_shape=[
            jax.ShapeDtypeStruct((m, d), F32),
            jax.ShapeDtypeStruct((batch, CONV_W - 1, e), F32),
        ],
        scratch_shapes=[pltpu.VMEM((tm + 8, e), F32)],
        compiler_params=_cparams(1),
        name="prompt_layer_b",
    )(h, nw, win, cw, wout)


def _c_proj_kernel(h_ref, nw_ref, w_ref, cos_ref, s1_ref, s2_ref, o_ref, *rest,
                   seq, tiles_per_seq):
    tails = rest[:N_DIL]
    stage_ref, sem = rest[N_DIL:]
    j = pl.program_id(0)
    i = pl.program_id(1)
    e = o_ref.shape[-1]
    heads = e // HEAD_DIM
    x = h_ref[...]
    tm = x.shape[0]
    z = _mm(_rms(x, nw_ref[...]), w_ref[...])

    @pl.when(j < 2 * N_DIL)
    def _():
        cos, s1, s2 = cos_ref[...], s1_ref[...], s2_ref[...]
        for h in range(heads):
            sl = slice(h * HEAD_DIM, (h + 1) * HEAD_DIM)
            stage_ref[:, sl] = _rope_head(z[:, sl], cos, s1, s2)

    @pl.when(j >= 2 * N_DIL)
    def _():
        stage_ref[...] = z

    o_ref[...] = stage_ref[...].astype(o_ref.dtype)

    b = i // tiles_per_seq
    ti = i % tiles_per_seq
    for kvi in range(2):
        for gi, (win, _) in enumerate(DIL_GROUPS):
            keep = min(win, seq)
            n_tail_tiles = pl.cdiv(keep, tm)
            cond = jnp.logical_and(j == N_DIL * (1 + kvi) + gi,
                                   ti >= tiles_per_seq - n_tail_tiles)

            @pl.when(cond)
            def _(kvi=kvi, gi=gi, keep=keep):
                cols = pl.ds(kvi * e, e)
                if keep >= tm:
                    src = stage_ref
                    dst = tails[gi].at[b, pl.ds(ti * tm - (seq - keep), tm), cols]
                else:
                    src = stage_ref.at[pl.ds(tm - keep, keep), :]
                    dst = tails[gi].at[b, :, cols]
                cp = pltpu.make_async_copy(src, dst, sem)
                cp.start()
                cp.wait()


def _c_proj(h, nw, win, cos, s1, s2, *, batch, e):
    m, d = h.shape
    seq = m // batch
    tm = TM_PROJ
    tiles_per_seq = seq // tm
    n_sec = win.shape[1] // e
    for w, _ in DIL_GROUPS:
        keep = min(w, seq)
        assert keep % tm == 0 or tm % keep == 0
    tab = pl.BlockSpec((tm, LANES), lambda j, i: (i % tiles_per_seq, 0))
    return pl.pallas_call(
        functools.partial(_c_proj_kernel, seq=seq, tiles_per_seq=tiles_per_seq),
        grid=(n_sec, m // tm),
        in_specs=[
            pl.BlockSpec((tm, d), lambda j, i: (i, 0)),
            _resident(nw.shape),
            pl.BlockSpec((d, e), lambda j, i: (0, j)),
            tab, tab, tab,
        ],
        out_specs=[pl.BlockSpec((None, tm, e), lambda j, i: (j, i, 0))]
        + [pl.BlockSpec(memory_space=pl.ANY)] * N_DIL,
        out_shape=[jax.ShapeDtypeStruct((n_sec, m, e), BF16)]
        + [jax.ShapeDtypeStruct((batch, min(w, seq), 2 * e), F32) for w, _ in DIL_GROUPS],
        scratch_shapes=[pltpu.VMEM((tm, e), F32), pltpu.SemaphoreType.DMA(())],
        compiler_params=_cparams(2),
        name="prompt_c_proj",
    )(h, nw, win, cos, s1, s2)


def _attn_kernel(q_ref, kp_ref, kc_ref, vp_ref, vc_ref, o_ref, lse_ref):
    blk = ATTN_BLK
    heads = q_ref.shape[-1] // HEAD_DIM
    n = pl.program_id(2)
    row = lax.broadcasted_iota(jnp.int32, (blk, 2 * blk), 0)
    col = lax.broadcasted_iota(jnp.int32, (blk, 2 * blk), 1)
    first_key = jnp.maximum(row, jnp.where(n > 0, 0, blk))
    valid = jnp.logical_and(col >= first_key, col <= row + blk)
    hlane = lax.broadcasted_iota(jnp.int32, (blk, heads), 1)
    lse = jnp.zeros((blk, heads), F32)
    scale = HEAD_DIM ** -0.5
    for h in range(heads):
        sl = slice(h * HEAD_DIM, (h + 1) * HEAD_DIM)
        k = jnp.concatenate([kp_ref[:, sl], kc_ref[:, sl]], axis=0)
        v = jnp.concatenate([vp_ref[:, sl], vc_ref[:, sl]], axis=0)
        s = lax.dot_general(q_ref[:, sl], k, (((1,), (1,)), ((), ())),
                            preferred_element_type=F32) * scale
        s = jnp.where(valid, s, -jnp.inf)
        mx = jnp.max(s, axis=-1, keepdims=True)
        p = jnp.exp(s - mx)
        den = jnp.sum(p, axis=-1, keepdims=True)
        o = jnp.dot(p.astype(BF16), v, preferred_element_type=F32) / den
        o_ref[:, sl] = o.astype(o_ref.dtype)
        lse = jnp.where(hlane == h, mx + jnp.log(den), lse)
    lse_ref[...] = lse


def _attn_group(qkvg, gi, *, batch):
    n_sec, m, e = qkvg.shape
    seq = m // batch
    _, dil = DIL_GROUPS[gi]
    blk = ATTN_BLK
    rows = seq // dil
    nb = rows // blk
    heads = e // HEAD_DIM
    view = qkvg.reshape(n_sec, batch, rows, dil * e)

    def spec(sec, prev):
        if prev:
            return pl.BlockSpec((None, None, blk, e),
                                lambda b, r, n: (sec, b, jnp.maximum(n - 1, 0), r))
        return pl.BlockSpec((None, None, blk, e), lambda b, r, n: (sec, b, n, r))

    qs, ks, vs = gi, N_DIL + gi, 2 * N_DIL + gi
    o, lse = pl.pallas_call(
        _attn_kernel,
        grid=(batch, dil, nb),
        in_specs=[spec(qs, False), spec(ks, True), spec(ks, False), spec(vs, True),
                  spec(vs, False)],
        out_specs=[
            pl.BlockSpec((None, blk, e), lambda b, r, n: (b, n, r)),
            pl.BlockSpec((None, None, blk, heads), lambda b, r, n: (b, r, n, 0)),
        ],
        out_shape=[
            jax.ShapeDtypeStruct((batch, rows, dil * e), BF16),
            jax.ShapeDtypeStruct((batch, dil, rows, heads), F32),
        ],
        compiler_params=_cparams(3),
        name=f"prompt_attn_g{gi}",
    )(view, view, view, view, view)
    o = o.reshape(m, e)
    lse = jnp.transpose(lse, (0, 2, 1, 3)).reshape(m, heads)
    return o, lse


def _group_weights(lses):
    mx = lses[0]
    for l in lses[1:]:
        mx = jnp.maximum(mx, l)
    ex = [jnp.exp(l - mx) for l in lses]
    den = ex[0]
    for x in ex[1:]:
        den = den + x
    return [x / den for x in ex]


def _c_finish_kernel(*refs):
    o_refs = refs[:N_DIL]
    l_refs = refs[N_DIL:2 * N_DIL]
    gate_ref, h_ref, wout_ref, out_ref, a_ref = refs[2 * N_DIL:]
    heads = l_refs[0].shape[-1]
    wts = _group_weights([l[...] for l in l_refs])
    for h in range(heads):
        sl = slice(h * HEAD_DIM, (h + 1) * HEAD_DIM)
        comb = wts[0][:, h:h + 1] * o_refs[0][:, sl].astype(F32)
        for gi in range(1, N_DIL):
            comb = comb + wts[gi][:, h:h + 1] * o_refs[gi][:, sl].astype(F32)
        a_ref[:, sl] = (comb * _silu(gate_ref[:, sl].astype(F32))).astype(a_ref.dtype)
    out_ref[...] = h_ref[...] + jnp.dot(a_ref[...], wout_ref[...], preferred_element_type=F32)


def _c_finish(os_, lses, qkvg, h, wout):
    m, d = h.shape
    e = wout.shape[0]
    heads = e // HEAD_DIM
    tm = TM_LAYER
    gate_sec = qkvg.shape[0] - 1
    return pl.pallas_call(
        _c_finish_kernel,
        grid=(m // tm,),
        in_specs=[pl.BlockSpec((tm, e), lambda i: (i, 0))] * N_DIL
        + [pl.BlockSpec((tm, heads), lambda i: (i, 0))] * N_DIL
        + [pl.BlockSpec((None, tm, e), lambda i: (gate_sec, i, 0)),
           pl.BlockSpec((tm, d), lambda i: (i, 0)),
           _resident(wout.shape)],
        out_specs=pl.BlockSpec((tm, d), lambda i: (i, 0)),
        out_shape=jax.ShapeDtypeStruct((m, d), F32),
        scratch_shapes=[pltpu.VMEM((tm, e), BF16)],
        compiler_params=_cparams(1),
        name="prompt_c_finish",
    )(*os_, *lses, qkvg, h, wout)


def _s_proj_kernel(h_ref, nw_ref, w_ref, o_ref):
    o_ref[...] = _mm(_rms(h_ref[...], nw_ref[...]), w_ref[...])


def _s_proj(h, nw, win, *, tn):
    bd, d = h.shape
    n = win.shape[1]
    return pl.pallas_call(
        _s_proj_kernel,
        grid=(n // tn,),
        in_specs=[
            pl.BlockSpec((bd, d), lambda j: (0, 0)),
            pl.BlockSpec(nw.shape, lambda j: (0, 0)),
            pl.BlockSpec((d, tn), lambda j: (0, j)),
        ],
        out_specs=pl.BlockSpec((bd, tn), lambda j: (0, j)),
        out_shape=jax.ShapeDtypeStruct((bd, n), F32),
        compiler_params=_cparams(1),
        name="sample_proj",
    )(h, nw, win)


def _s_out(a, h, wout, fnw, final_norm):
    out = h + _mm(a, wout)
    return _rms(out, fnw) if final_norm else out


def _s_a_kernel(z_ref, lng_ref, lnb_ref, w0_ref, b0_ref, h_ref, wout_ref, fnw_ref,
                o_ref, v_ref, *, final_norm):
    e = lng_ref.shape[-1]
    z = z_ref[...]
    vn = _layernorm(z[:, e:2 * e], lng_ref[...], lnb_ref[...])
    v_ref[...] = vn
    a = z[:, :e] * (w0_ref[...] * vn + b0_ref[...]) * _silu(z[:, 2 * e:])
    o_ref[...] = _s_out(a, h_ref[...], wout_ref[...], fnw_ref[...], final_norm)


def _s_a(z, lng, lnb, w0, b0, h, wout, fnw, *, final_norm):
    bd, d = h.shape
    e = lng.shape[-1]
    return pl.pallas_call(
        functools.partial(_s_a_kernel, final_norm=final_norm),
        out_shape=[jax.ShapeDtypeStruct((bd, d), F32), jax.ShapeDtypeStruct((bd, e), F32)],
        compiler_params=pltpu.CompilerParams(vmem_limit_bytes=VMEM_LIMIT_BYTES),
        name="sample_layer_a",
    )(z, lng, lnb, w0, b0, h, wout, fnw)


def _s_b_kernel(z_ref, st0_ref, st1_ref, cw_ref, h_ref, wout_ref, o_ref, u_ref):
    e = cw_ref.shape[-1]
    z = z_ref[...]
    u = z[:, e:2 * e] * z[:, 2 * e:3 * e]
    u_ref[...] = u
    conv = cw_ref[0:1, :] * st0_ref[...] + cw_ref[1:2, :] * st1_ref[...] + cw_ref[2:3, :] * u
    a = z[:, :e] * conv * _silu(z[:, 3 * e:])
    o_ref[...] = h_ref[...] + _mm(a, wout_ref[...])


def _s_b(z, st0, st1, cw, h, wout):
    bd, d = h.shape
    e = cw.shape[-1]
    return pl.pallas_call(
        _s_b_kernel,
        out_shape=[jax.ShapeDtypeStruct((bd, d), F32), jax.ShapeDtypeStruct((bd, e), F32)],
        compiler_params=pltpu.CompilerParams(vmem_limit_bytes=VMEM_LIMIT_BYTES),
        name="sample_layer_b",
    )(z, st0, st1, cw, h, wout)


def _s_attn_kernel(z_ref, cos_ref, s1_ref, s2_ref, *rest, e):
    caches = rest[:N_DIL]
    o_ref = rest[N_DIL]
    kv_refs = rest[N_DIL + 1:]
    heads = e // HEAD_DIM
    cos, s1, s2 = cos_ref[...], s1_ref[...], s2_ref[...]
    scale = HEAD_DIM ** -0.5
    outs = [[None] * heads for _ in range(N_DIL)]
    lses = [[None] * heads for _ in range(N_DIL)]
    for gi in range(N_DIL):
        qo, ko, vo = gi * e, (N_DIL + gi) * e, (2 * N_DIL + gi) * e
        kc = caches[gi][:, :e]
        vc = caches[gi][:, e:]
        for h in range(heads):
            hs = h * HEAD_DIM
            q = _rope_head(z_ref[:, qo + hs:qo + hs + HEAD_DIM], cos, s1, s2)
            kn = _rope_head(z_ref[:, ko + hs:ko + hs + HEAD_DIM], cos, s1, s2)
            vn = z_ref[:, vo + hs:vo + hs + HEAD_DIM]
            kv_refs[gi][:, hs:hs + HEAD_DIM] = kn
            kv_refs[gi][:, e + hs:e + hs + HEAD_DIM] = vn
            sl = slice(hs, hs + HEAD_DIM)
            s_c = jnp.sum(kc[:, sl] * q, axis=-1, keepdims=True) * scale
            s_n = jnp.sum(kn * q, axis=-1, keepdims=True) * scale
            mx = jnp.maximum(jnp.max(s_c, axis=0, keepdims=True), s_n)
            p_c = jnp.exp(s_c - mx)
            p_n = jnp.exp(s_n - mx)
            den = jnp.sum(p_c, axis=0, keepdims=True) + p_n
            o = (jnp.sum(p_c * vc[:, sl], axis=0, keepdims=True) + p_n * vn) / den
            outs[gi][h] = o
            lses[gi][h] = mx + jnp.log(den)
    for h in range(heads):
        wts = _group_weights([lses[gi][h] for gi in range(N_DIL)])
        comb = wts[0] * outs[0][h]
        for gi in range(1, N_DIL):
            comb = comb + wts[gi] * outs[gi][h]
        o_ref[:, h * HEAD_DIM:(h + 1) * HEAD_DIM] = comb


def _s_attn(z, cos, s1, s2, caches, *, e):
    bd, n = z.shape
    blk = ATTN_BLK
    views = []
    specs = []
    for (win, dil), c in zip(DIL_GROUPS, caches):
        lb = c.shape[1]
        assert lb == win and lb == blk * dil, "cache must hold one full window"
        views.append(c.reshape(bd, blk, dil * 2 * e))
        specs.append(pl.BlockSpec((None, blk, 2 * e), lambda b: (b, 0, 0)))
    tab = pl.BlockSpec((1, LANES), lambda b: (0, 0))
    row = lambda width: pl.BlockSpec((None, 1, width), lambda b: (b, 0, 0))
    res = pl.pallas_call(
        functools.partial(_s_attn_kernel, e=e),
        grid=(bd,),
        in_specs=[row(n), tab, tab, tab] + specs,
        out_specs=[row(e)] + [row(2 * e)] * N_DIL,
        out_shape=[jax.ShapeDtypeStruct((bd, 1, e), F32)]
        + [jax.ShapeDtypeStruct((bd, 1, 2 * e), F32)] * N_DIL,
        compiler_params=_cparams(1),
        name="sample_attn",
    )(z.reshape(bd, 1, n), cos, s1, s2, *views)
    return res[0].reshape(bd, e), res[1:]


def _s_c_out_kernel(o_ref, g_ref, h_ref, wout_ref, out_ref):
    a = o_ref[...] * _silu(g_ref[...])
    out_ref[...] = h_ref[...] + _mm(a, wout_ref[...])


def _s_c_out(o, z, h, wout):
    bd, d = h.shape
    e = o.shape[-1]
    gate_blk = z.shape[1] // e - 1
    return pl.pallas_call(
        _s_c_out_kernel,
        grid=(1,),
        in_specs=[
            pl.BlockSpec((bd, e), lambda i: (0, 0)),
            pl.BlockSpec((bd, e), lambda i: (0, gate_blk)),
            pl.BlockSpec((bd, d), lambda i: (0, 0)),
            pl.BlockSpec(wout.shape, lambda i: (0, 0)),
        ],
        out_specs=pl.BlockSpec((bd, d), lambda i: (0, 0)),
        out_shape=jax.ShapeDtypeStruct((bd, d), F32),
        compiler_params=_cparams(1),
        name="sample_c_out",
    )(o, z, h, wout)


def _rope_tables(pos):
    half = ROT_DIM // 2
    inv = ROPE_THETA ** (-jnp.arange(half, dtype=F32) * (2.0 / ROT_DIM))
    ang = pos.astype(F32)[:, None] * inv[None, :]
    cos, sin = jnp.cos(ang), jnp.sin(ang)
    n = pos.shape[0]
    rest = HEAD_DIM - ROT_DIM
    cosf = jnp.concatenate([cos, cos, jnp.ones((n, rest), F32)], axis=1)
    s1 = jnp.concatenate([-sin, jnp.zeros((n, half + rest), F32)], axis=1)
    s2 = jnp.concatenate([jnp.zeros((n, half), F32), sin, jnp.zeros((n, rest), F32)], axis=1)
    return cosf, s1, s2


def kernel(x_prompt, x_sample, state_conv, cache_kv_w128, cache_kv_w512, cache_kv_w2048,
           norm_w, final_norm_w, a_w_in, a_ln_g, a_ln_b, a_w_s, a_b_s, a_w_out,
           b_w_in, b_conv_w, b_w_out, c_w_in, c_w_out):
    caches = (cache_kv_w128, cache_kv_w512, cache_kv_w2048)
    batch, seq, d = x_prompt.shape
    bd, t_dec, _ = x_sample.shape
    assert t_dec == 1, "sample kernels handle one new token per sequence"
    depth = norm_w.shape[0]
    e = a_ln_g.shape[-1]
    ge = e // A_GROUPS
    heads = e // HEAD_DIM
    m = batch * seq

    hp = x_prompt.reshape(m, d)
    hs = x_sample.reshape(bd, d)
    fnw = final_norm_w.reshape(1, d)
    cos_p, s1_p, s2_p = _rope_tables(jnp.arange(seq))
    cos_s, s1_s, s2_s = _rope_tables(PAST_LEN + jnp.arange(t_dec))

    v_rows, conv_p, conv_s = [], [], []
    kv_p = [[] for _ in DIL_GROUPS]
    kv_s = [[] for _ in DIL_GROUPS]
    for i in range(depth):
        kind, j = i % N_MIXERS, i // N_MIXERS
        nw = norm_w[i].reshape(1, d)
        last = i == depth - 1
        if kind == 0:
            win = a_w_in[j].astype(BF16)
            wout = a_w_out[j].astype(BF16)
            lng = a_ln_g[j].reshape(1, e)
            lnb = a_ln_b[j].reshape(1, e)
            ws = a_w_s[j].astype(BF16)
            bias = jnp.repeat(a_b_s[j].T, ge, axis=1)
            hp = _layer_a(hp, nw, win, lng, lnb, ws, bias, wout, fnw, final_norm=last)
            z = _s_proj(hs, nw, win, tn=e)
            w0 = jnp.repeat(a_w_s[j][:, 0, 0], ge).reshape(1, e)
            b0 = jnp.repeat(a_b_s[j][:, 0], ge).reshape(1, e)
            hs, v_new = _s_a(z, lng, lnb, w0, b0, hs, wout, fnw, final_norm=last)
            v_rows.append(v_new.reshape(bd, t_dec, e))
        elif kind == 1:
            win = b_w_in[j].astype(BF16)
            wout = b_w_out[j].astype(BF16)
            cw = b_conv_w[j]
            hp, st = _layer_b(hp, nw, win, cw, wout, batch=batch)
            conv_p.append(st)
            z = _s_proj(hs, nw, win, tn=e)
            st0, st1 = state_conv[j][:, 0], state_conv[j][:, 1]
            hs, u_new = _s_b(z, st0, st1, cw, hs, wout)
            conv_s.append(jnp.stack([st1, u_new], axis=1))
        else:
            win = c_w_in[j].astype(BF16)
            wout = c_w_out[j].astype(BF16)
            qkvg, *tails = _c_proj(hp, nw, win, cos_p, s1_p, s2_p, batch=batch, e=e)
            os_, lses = zip(*[_attn_group(qkvg, gi, batch=batch) for gi in range(N_DIL)])
            hp = _c_finish(os_, lses, qkvg, hp, wout)
            for gi, t in enumerate(tails):
                kv_p[gi].append(t.reshape(batch, t.shape[1], 2, heads, HEAD_DIM))
            z = _s_proj(hs, nw, win, tn=e)
            o_s, kv_new = _s_attn(z, cos_s, s1_s, s2_s, [c[j] for c in caches], e=e)
            hs = _s_c_out(o_s, z, hs, wout)
            for gi, t in enumerate(kv_new):
                kv_s[gi].append(t.reshape(bd, t_dec, 2, heads, HEAD_DIM))
        if last and kind != 0:
            raise NotImplementedError("final norm is fused into the gMLP layer kernels")

    y_prompt = hp.reshape(batch, seq, d)
    y_sample = hs.reshape(bd, t_dec, d)
    kvp = [jnp.stack(l) for l in kv_p]
    kvs = [jnp.stack(l) for l in kv_s]
    return (y_prompt, y_sample, jnp.stack(v_rows), jnp.stack(conv_p), jnp.stack(conv_s),
            kvp[0], kvs[0], kvp[1], kvs[1], kvp[2], kvs[2])
```

```python
import functools

import jax
import jax.numpy as jnp
from jax import lax
from jax.experimental import pallas as pl
from jax.experimental.pallas import tpu as pltpu

F32 = jnp.float32
BF16 = jnp.bfloat16

N_MIXERS = 3
CHUNK = 128
A_GROUPS = 8
CONV_W = 3
HEAD_DIM = 128
DIL_GROUPS = ((128, 1), (512, 4), (2048, 16))
N_DIL = len(DIL_GROUPS)
ROT_DIM = HEAD_DIM // 4
ROPE_THETA = 500000.0
PAST_LEN = 8192
EPS = 1e-6
ATTN_BLK = 128

VMEM_LIMIT_BYTES = 56 * 1024 * 1024
LANES = 128
MXU_N = 256

TM_LAYER = 256
TM_PROJ = 512


def _cparams(n_axes):
    return pltpu.CompilerParams(
        dimension_semantics=("arbitrary",) * n_axes,
        vmem_limit_bytes=VMEM_LIMIT_BYTES,
    )


def _resident(shape):
    nd = len(shape)
    return pl.BlockSpec(shape, lambda *_: (0,) * nd, pipeline_mode=pl.Buffered(1))


def _rms(x, w):
    return x * lax.rsqrt(jnp.mean(x * x, axis=-1, keepdims=True) + EPS) * w


def _layernorm(v, g, b):
    vc = v - jnp.mean(v, axis=-1, keepdims=True)
    return vc * lax.rsqrt(jnp.mean(vc * vc, axis=-1, keepdims=True) + EPS) * g + b


def _silu(g):
    return g * (1.0 / (1.0 + jnp.exp(-g)))


def _mm(a, w):
    return jnp.dot(a.astype(BF16), w, preferred_element_type=F32)


def _rope_head(zh, cos, s1, s2):
    up = pltpu.roll(zh, HEAD_DIM - ROT_DIM // 2, axis=1)
    dn = pltpu.roll(zh, ROT_DIM // 2, axis=1)
    return zh * cos + up * s1 + dn * s2


def _layer_a_kernel(h_ref, nw_ref, win_ref, lng_ref, lnb_ref, ws_ref, bias_ref, wout_ref,
                    fnw_ref, o_ref, mix_ref, *, final_norm):
    e = lng_ref.shape[-1]
    ge = e // A_GROUPS
    x = h_ref[...]
    tm = x.shape[0]
    z = _mm(_rms(x, nw_ref[...]), win_ref[...])
    u = z[:, :e]
    g = z[:, 2 * e:]
    vb = _layernorm(z[:, e:2 * e], lng_ref[...], lnb_ref[...]).astype(BF16)
    row = lax.broadcasted_iota(jnp.int32, (CHUNK, CHUNK), 0)
    col = lax.broadcasted_iota(jnp.int32, (CHUNK, CHUNK), 1)
    for grp in range(A_GROUPS):
        w = jnp.where(row >= col, ws_ref[grp], jnp.zeros((), BF16))
        cs = slice(grp * ge, (grp + 1) * ge)
        for c in range(tm // CHUNK):
            rs = slice(c * CHUNK, (c + 1) * CHUNK)
            mix_ref[rs, cs] = (
                jnp.dot(w, vb[rs, cs], preferred_element_type=F32) + bias_ref[:, cs])
    a = u * mix_ref[...] * _silu(g)
    out = x + _mm(a, wout_ref[...])
    if final_norm:
        out = _rms(out, fnw_ref[...])
    o_ref[...] = out


def _layer_a(h, nw, win, lng, lnb, ws, bias, wout, fnw, *, final_norm):
    m, d = h.shape
    e = lng.shape[-1]
    tm = TM_LAYER
    return pl.pallas_call(
        functools.partial(_layer_a_kernel, final_norm=final_norm),
        grid=(m // tm,),
        in_specs=[
            pl.BlockSpec((tm, d), lambda i: (i, 0)),
            _resident(nw.shape), _resident(win.shape), _resident(lng.shape),
            _resident(lnb.shape), _resident(ws.shape), _resident(bias.shape),
            _resident(wout.shape), _resident(fnw.shape),
        ],
        out_specs=pl.BlockSpec((tm, d), lambda i: (i, 0)),
        out_shape=jax.ShapeDtypeStruct((m, d), F32),
        scratch_shapes=[pltpu.VMEM((tm, e), F32)],
        compiler_params=_cparams(1),
        name="prompt_layer_a",
    )(h, nw, win, lng, lnb, ws, bias, wout, fnw)


def _layer_b_kernel(h_ref, nw_ref, win_ref, cw_ref, wout_ref, o_ref, st_ref, ubuf_ref, *,
                    tiles_per_seq):
    e = cw_ref.shape[-1]
    halo = 8
    i = pl.program_id(0)

    @pl.when(i % tiles_per_seq == 0)
    def _():
        ubuf_ref[0:halo, :] = jnp.zeros((halo, e), F32)

    x = h_ref[...]
    tm = x.shape[0]
    z = _mm(_rms(x, nw_ref[...]), win_ref[...])
    bg = z[:, :e]
    u = z[:, e:2 * e] * z[:, 2 * e:3 * e]
    g = z[:, 3 * e:]
    ubuf_ref[halo:halo + tm, :] = u
    conv = (cw_ref[0:1, :] * ubuf_ref[halo - 2:halo - 2 + tm, :]
            + cw_ref[1:2, :] * ubuf_ref[halo - 1:halo - 1 + tm, :]
            + cw_ref[2:3, :] * u)
    a = bg * conv * _silu(g)
    o_ref[...] = x + _mm(a, wout_ref[...])
    st_ref[...] = ubuf_ref[halo + tm - (CONV_W - 1):halo + tm, :]
    ubuf_ref[0:halo, :] = ubuf_ref[tm:tm + halo, :]


def _layer_b(h, nw, win, cw, wout, *, batch):
    m, d = h.shape
    e = cw.shape[-1]
    tm = TM_LAYER
    tiles_per_seq = m // batch // tm
    return pl.pallas_call(
        functools.partial(_layer_b_kernel, tiles_per_seq=tiles_per_seq),
        grid=(m // tm,),
        in_specs=[
            pl.BlockSpec((tm, d), lambda i: (i, 0)),
            _resident(nw.shape), _resident(win.shape), _resident(cw.shape),
            _resident(wout.shape),
        ],
        out_specs=[
            pl.BlockSpec((tm, d), lambda i: (i, 0)),
            pl.BlockSpec((None, CONV_W - 1, e), lambda i: (i // tiles_per_seq, 0, 0)),
        ],
        out_shape=[
            jax.ShapeDtypeStruct((m, d), F32),
            jax.ShapeDtypeStruct((batch, CONV_W - 1, e), F32),
        ],
        scratch_shapes=[pltpu.VMEM((tm + 8, e), F32)],
        compiler_params=_cparams(1),
        name="prompt_layer_b",
    )(h, nw, win, cw, wout)


def _c_prep_kernel(h_ref, nw_ref, n0_ref, *rest, dils):
    outs, slab_ref = rest[:-1], rest[-1]
    n = _rms(h_ref[...], nw_ref[...])
    tm, d = n.shape
    n0_ref[...] = n.astype(n0_ref.dtype)
    n_slab = d // LANES
    for c in range(n_slab):
        slab_ref[c] = n[:, c * LANES:(c + 1) * LANES]
    for o_ref, dil in zip(outs, dils):
        rows = tm // dil
        for r in range(dil):
            for c in range(n_slab):
                o_ref[r, :, c * LANES:(c + 1) * LANES] = (
                    slab_ref[c, pl.ds(r, rows, stride=dil), :].astype(o_ref.dtype))


def _c_prep(h, nw, *, batch):
    m, d = h.shape
    seq = m // batch
    tm = TM_PROJ
    tps = seq // tm
    dils = tuple(dil for _, dil in DIL_GROUPS if dil > 1)
    res = pl.pallas_call(
        functools.partial(_c_prep_kernel, dils=dils),
        grid=(m // tm,),
        in_specs=[pl.BlockSpec((tm, d), lambda i: (i, 0)), _resident(nw.shape)],
        out_specs=[pl.BlockSpec((tm, d), lambda i: (i, 0))]
        + [pl.BlockSpec((None, dil, tm // dil, d), lambda i: (i // tps, 0, i % tps, 0))
           for dil in dils],
        out_shape=[jax.ShapeDtypeStruct((m, d), BF16)]
        + [jax.ShapeDtypeStruct((batch, dil, seq // dil, d), BF16) for dil in dils],
        scratch_shapes=[pltpu.VMEM((d // LANES, tm, LANES), F32)],
        compiler_params=_cparams(1),
        name="prompt_c_prep",
    )(h, nw)
    by_dil = dict(zip(dils, res[1:]))
    return [res[0] if dil == 1 else by_dil[dil].reshape(m, d) for _, dil in DIL_GROUPS]


def _c_proj_kernel(n_ref, w_ref, cos_ref, s1_ref, s2_ref, o_ref, *, n_rope):
    j = pl.program_id(0)
    e = o_ref.shape[-1]
    n = n_ref[...]

    def section(rope):
        if rope:
            cos, s1, s2 = cos_ref[...], s1_ref[...], s2_ref[...]
        for c in range(e // MXU_N):
            z = jnp.dot(n, w_ref[:, c * MXU_N:(c + 1) * MXU_N], preferred_element_type=F32)
            for hh in range(MXU_N // HEAD_DIM):
                zh = z[:, hh * HEAD_DIM:(hh + 1) * HEAD_DIM]
                if rope:
                    zh = _rope_head(zh, cos, s1, s2)
                lo = c * MXU_N + hh * HEAD_DIM
                o_ref[:, lo:lo + HEAD_DIM] = zh.astype(o_ref.dtype)

    pl.when(j < n_rope)(functools.partial(section, True))
    pl.when(j >= n_rope)(functools.partial(section, False))


def _c_proj(n, win, tabs, gi, *, e):
    m, d = n.shape
    seq = tabs[0].shape[0]
    tm = TM_PROJ
    tps = seq // tm
    gate_sec = win.shape[1] // e - 1
    n_sec = 4 if gi == 0 else 3
    tab = pl.BlockSpec((tm, LANES), lambda j, i: (i % tps, 0))
    return pl.pallas_call(
        functools.partial(_c_proj_kernel, n_rope=2),
        grid=(n_sec, m // tm),
        in_specs=[
            pl.BlockSpec((tm, d), lambda j, i: (i, 0)),
            pl.BlockSpec((d, e), lambda j, i: (0, jnp.where(j < 3, N_DIL * j + gi, gate_sec))),
            tab, tab, tab,
        ],
        out_specs=pl.BlockSpec((None, tm, e), lambda j, i: (j, i, 0)),
        out_shape=jax.ShapeDtypeStruct((n_sec, m, e), BF16),
        compiler_params=_cparams(2),
        name=f"prompt_c_proj_g{gi}",
    )(n, win, *tabs)


def _c_tail(n0, win, tabs, gi, *, batch, e):
    m, d = n0.shape
    seq = m // batch
    keep = min(DIL_GROUPS[gi][0], seq)
    tm = min(keep, TM_PROJ)
    assert keep % tm == 0 and (seq - keep) % tm == 0
    first = (seq - keep) // tm
    tab = pl.BlockSpec((tm, LANES), lambda j, b, i: (first + i, 0))
    return pl.pallas_call(
        functools.partial(_c_proj_kernel, n_rope=1),
        grid=(2, batch, keep // tm),
        in_specs=[
            pl.BlockSpec((tm, d), lambda j, b, i: (b * (seq // tm) + first + i, 0)),
            pl.BlockSpec((d, e), lambda j, b, i: (0, N_DIL * (1 + j) + gi)),
            tab, tab, tab,
        ],
        out_specs=pl.BlockSpec((None, tm, e), lambda j, b, i: (b, i, j)),
        out_shape=jax.ShapeDtypeStruct((batch, keep, 2 * e), F32),
        compiler_params=_cparams(3),
        name=f"prompt_c_tail_g{gi}",
    )(n0, win, *tabs)


def _attn_kernel(q_ref, kp_ref, kc_ref, vp_ref, vc_ref, o_ref, lse_ref):
    blk = ATTN_BLK
    heads = q_ref.shape[-1] // HEAD_DIM
    n = pl.program_id(2)
    row = lax.broadcasted_iota(jnp.int32, (blk, 2 * blk), 0)
    col = lax.broadcasted_iota(jnp.int32, (blk, 2 * blk), 1)
    first_key = jnp.maximum(row, jnp.where(n > 0, 0, blk))
    valid = jnp.logical_and(col >= first_key, col <= row + blk)
    hlane = lax.broadcasted_iota(jnp.int32, (blk, heads), 1)
    lse = jnp.zeros((blk, heads), F32)
    scale = HEAD_DIM ** -0.5
    for h in range(heads):
        sl = slice(h * HEAD_DIM, (h + 1) * HEAD_DIM)
        k = jnp.concatenate([kp_ref[:, sl], kc_ref[:, sl]], axis=0)
        v = jnp.concatenate([vp_ref[:, sl], vc_ref[:, sl]], axis=0)
        s = lax.dot_general(q_ref[:, sl], k, (((1,), (1,)), ((), ())),
                            preferred_element_type=F32) * scale
        s = jnp.where(valid, s, -jnp.inf)
        mx = jnp.max(s, axis=-1, keepdims=True)
        p = jnp.exp(s - mx)
        den = jnp.sum(p, axis=-1, keepdims=True)
        o = jnp.dot(p.astype(BF16), v, preferred_element_type=F32) / den
        o_ref[:, sl] = o.astype(o_ref.dtype)
        lse = jnp.where(hlane == h, mx + jnp.log(den), lse)
    lse_ref[...] = lse


def _attn_group(qkv, gi, *, batch):
    n_sec, m, e = qkv.shape
    seq = m // batch
    _, dil = DIL_GROUPS[gi]
    blk = ATTN_BLK
    rows = seq // dil
    nb = rows // blk
    heads = e // HEAD_DIM
    view = qkv.reshape(n_sec, batch, dil, rows, e)

    def spec(sec, prev):
        if prev:
            return pl.BlockSpec((None, None, None, blk, e),
                                lambda b, r, n: (sec, b, r, jnp.maximum(n - 1, 0), 0))
        return pl.BlockSpec((None, None, None, blk, e), lambda b, r, n: (sec, b, r, n, 0))

    return pl.pallas_call(
        _attn_kernel,
        grid=(batch, dil, nb),
        in_specs=[spec(0, False), spec(1, True), spec(1, False), spec(2, True),
                  spec(2, False)],
        out_specs=[
            pl.BlockSpec((None, None, blk, e), lambda b, r, n: (b, r, n, 0)),
            pl.BlockSpec((None, None, blk, heads), lambda b, r, n: (b, r, n, 0)),
        ],
        out_shape=[
            jax.ShapeDtypeStruct((batch, dil, rows, e), BF16),
            jax.ShapeDtypeStruct((batch, dil, rows, heads), F32),
        ],
        compiler_params=_cparams(3),
        name=f"prompt_attn_g{gi}",
    )(view, view, view, view, view)


def _group_weights(lses):
    mx = lses[0]
    for l in lses[1:]:
        mx = jnp.maximum(mx, l)
    ex = [jnp.exp(l - mx) for l in lses]
    den = ex[0]
    for x in ex[1:]:
        den = den + x
    return [x / den for x in ex]


def _c_finish_kernel(*refs, dils):
    o_refs = refs[:N_DIL]
    l_refs = refs[N_DIL:2 * N_DIL]
    gate_ref, h_ref, wout_ref, out_ref, a_ref = refs[2 * N_DIL:2 * N_DIL + 5]
    slabs = refs[2 * N_DIL + 5:]
    heads = l_refs[0].shape[-1]
    tm = h_ref.shape[0]
    slab_of = {}
    for gi, dil in enumerate(dils):
        if dil == 1:
            continue
        slab = slabs[len(slab_of)]
        slab_of[gi] = slab
        rows = tm // dil
        for r in range(dil):
            for h in range(heads):
                slab[h, pl.ds(r, rows, stride=dil), :] = (
                    o_refs[gi][r, :, h * HEAD_DIM:(h + 1) * HEAD_DIM].astype(F32))
    wts = _group_weights([l[...] for l in l_refs])
    for h in range(heads):
        sl = slice(h * HEAD_DIM, (h + 1) * HEAD_DIM)
        comb = None
        for gi in range(N_DIL):
            og = slab_of[gi][h] if gi in slab_of else o_refs[gi][:, sl].astype(F32)
            term = wts[gi][:, h:h + 1] * og
            comb = term if comb is None else comb + term
        a_ref[:, sl] = (comb * _silu(gate_ref[:, sl].astype(F32))).astype(a_ref.dtype)
    out_ref[...] = h_ref[...] + jnp.dot(a_ref[...], wout_ref[...], preferred_element_type=F32)


def _c_finish(os_, lses, qkv0, h, wout, *, batch):
    m, d = h.shape
    e = wout.shape[0]
    heads = e // HEAD_DIM
    seq = m // batch
    tm = TM_LAYER
    tps = seq // tm
    dils = tuple(dil for _, dil in DIL_GROUPS)
    gate_sec = qkv0.shape[0] - 1
    o_specs = [
        pl.BlockSpec((None, None, tm, e), lambda i: (i // tps, 0, i % tps, 0)) if dil == 1 else
        pl.BlockSpec((None, dil, tm // dil, e), lambda i: (i // tps, 0, i % tps, 0))
        for dil in dils]
    return pl.pallas_call(
        functools.partial(_c_finish_kernel, dils=dils),
        grid=(m // tm,),
        in_specs=o_specs
        + [pl.BlockSpec((tm, heads), lambda i: (i, 0))] * N_DIL
        + [pl.BlockSpec((None, tm, e), lambda i: (gate_sec, i, 0)),
           pl.BlockSpec((tm, d), lambda i: (i, 0)),
           _resident(wout.shape)],
        out_specs=pl.BlockSpec((tm, d), lambda i: (i, 0)),
        out_shape=jax.ShapeDtypeStruct((m, d), F32),
        scratch_shapes=[pltpu.VMEM((tm, e), BF16)]
        + [pltpu.VMEM((heads, tm, HEAD_DIM), F32) for dil in dils if dil > 1],
        compiler_params=_cparams(1),
        name="prompt_c_finish",
    )(*os_, *lses, qkv0, h, wout)


def _s_proj_kernel(h_ref, nw_ref, w_ref, o_ref):
    o_ref[...] = _mm(_rms(h_ref[...], nw_ref[...]), w_ref[...])


def _s_proj(h, nw, win, *, tn):
    bd, d = h.shape
    n = win.shape[1]
    return pl.pallas_call(
        _s_proj_kernel,
        grid=(n // tn,),
        in_specs=[
            pl.BlockSpec((bd, d), lambda j: (0, 0)),
            pl.BlockSpec(nw.shape, lambda j: (0, 0)),
            pl.BlockSpec((d, tn), lambda j: (0, j)),
        ],
        out_specs=pl.BlockSpec((bd, tn), lambda j: (0, j)),
        out_shape=jax.ShapeDtypeStruct((bd, n), F32),
        compiler_params=_cparams(1),
        name="sample_proj",
    )(h, nw, win)


def _s_out(a, h, wout, fnw, final_norm):
    out = h + _mm(a, wout)
    return _rms(out, fnw) if final_norm else out


def _s_a_kernel(z_ref, lng_ref, lnb_ref, w0_ref, b0_ref, h_ref, wout_ref, fnw_ref,
                o_ref, v_ref, *, final_norm):
    e = lng_ref.shape[-1]
    z = z_ref[...]
    vn = _layernorm(z[:, e:2 * e], lng_ref[...], lnb_ref[...])
    v_ref[...] = vn
    a = z[:, :e] * (w0_ref[...] * vn + b0_ref[...]) * _silu(z[:, 2 * e:])
    o_ref[...] = _s_out(a, h_ref[...], wout_ref[...], fnw_ref[...], final_norm)


def _s_a(z, lng, lnb, w0, b0, h, wout, fnw, *, final_norm):
    bd, d = h.shape
    e = lng.shape[-1]
    return pl.pallas_call(
        functools.partial(_s_a_kernel, final_norm=final_norm),
        out_shape=[jax.ShapeDtypeStruct((bd, d), F32), jax.ShapeDtypeStruct((bd, e), F32)],
        compiler_params=pltpu.CompilerParams(vmem_limit_bytes=VMEM_LIMIT_BYTES),
        name="sample_layer_a",
    )(z, lng, lnb, w0, b0, h, wout, fnw)


def _s_b_kernel(z_ref, st0_ref, st1_ref, cw_ref, h_ref, wout_ref, o_ref, u_ref):
    e = cw_ref.shape[-1]
    z = z_ref[...]
    u = z[:, e:2 * e] * z[:, 2 * e:3 * e]
    u_ref[...] = u
    conv = cw_ref[0:1, :] * st0_ref[...] + cw_ref[1:2, :] * st1_ref[...] + cw_ref[2:3, :] * u
    a = z[:, :e] * conv * _silu(z[:, 3 * e:])
    o_ref[...] = h_ref[...] + _mm(a, wout_ref[...])


def _s_b(z, st0, st1, cw, h, wout):
    bd, d = h.shape
    e = cw.shape[-1]
    return pl.pallas_call(
        _s_b_kernel,
        out_shape=[jax.ShapeDtypeStruct((bd, d), F32), jax.ShapeDtypeStruct((bd, e), F32)],
        compiler_params=pltpu.CompilerParams(vmem_limit_bytes=VMEM_LIMIT_BYTES),
        name="sample_layer_b",
    )(z, st0, st1, cw, h, wout)


def _s_attn_kernel(z_ref, cos_ref, s1_ref, s2_ref, *rest, heads):
    caches = rest[:N_DIL]
    o_ref = rest[N_DIL]
    kv_refs = rest[N_DIL + 1:]
    cos, s1, s2 = cos_ref[...], s1_ref[...], s2_ref[...]
    scale = HEAD_DIM ** -0.5
    outs, lses = [], []
    for gi in range(N_DIL):
        sec = lambda s: z_ref[(s * N_DIL + gi) * heads:(s * N_DIL + gi + 1) * heads, :]
        q = _rope_head(sec(0), cos, s1, s2)
        kn = _rope_head(sec(1), cos, s1, s2)
        vn = sec(2)
        kv_refs[gi][0] = kn
        kv_refs[gi][1] = vn
        kc = caches[gi][:, 0]
        vc = caches[gi][:, 1]
        s_c = jnp.sum(kc * q[None], axis=-1, keepdims=True) * scale
        s_n = jnp.sum(kn * q, axis=-1, keepdims=True) * scale
        mx = jnp.maximum(jnp.max(s_c, axis=0), s_n)
        p_c = jnp.exp(s_c - mx[None])
        p_n = jnp.exp(s_n - mx)
        den = jnp.sum(p_c, axis=0) + p_n
        outs.append((jnp.sum(p_c * vc, axis=0) + p_n * vn) / den)
        lses.append(mx + jnp.log(den))
    wts = _group_weights(lses)
    comb = wts[0] * outs[0]
    for gi in range(1, N_DIL):
        comb = comb + wts[gi] * outs[gi]
    o_ref[...] = comb


def _s_attn(z, cos, s1, s2, caches, *, e):
    bd, n = z.shape
    blk = ATTN_BLK
    heads = e // HEAD_DIM
    views = []
    specs = []
    for (win, dil), c in zip(DIL_GROUPS, caches):
        lb = c.shape[1]
        assert lb == win and lb == blk * dil, "cache must hold one full window"
        views.append(c.reshape(bd, blk, dil, 2, heads, HEAD_DIM))
        specs.append(pl.BlockSpec((None, blk, None, 2, heads, HEAD_DIM),
                                  lambda b: (b, 0, 0, 0, 0, 0)))
    tab = pl.BlockSpec((1, LANES), lambda b: (0, 0))
    res = pl.pallas_call(
        functools.partial(_s_attn_kernel, heads=heads),
        grid=(bd,),
        in_specs=[pl.BlockSpec((None, n // HEAD_DIM, HEAD_DIM), lambda b: (b, 0, 0)),
                  tab, tab, tab] + specs,
        out_specs=[pl.BlockSpec((None, heads, HEAD_DIM), lambda b: (b, 0, 0))]
        + [pl.BlockSpec((None, 2, heads, HEAD_DIM), lambda b: (b, 0, 0, 0))] * N_DIL,
        out_shape=[jax.ShapeDtypeStruct((bd, heads, HEAD_DIM), F32)]
        + [jax.ShapeDtypeStruct((bd, 2, heads, HEAD_DIM), F32)] * N_DIL,
        compiler_params=_cparams(1),
        name="sample_attn",
    )(z.reshape(bd, n // HEAD_DIM, HEAD_DIM), cos, s1, s2, *views)
    return res[0].reshape(bd, e), res[1:]


def _s_c_out_kernel(o_ref, g_ref, h_ref, wout_ref, out_ref):
    a = o_ref[...] * _silu(g_ref[...])
    out_ref[...] = h_ref[...] + _mm(a, wout_ref[...])


def _s_c_out(o, z, h, wout):
    bd, d = h.shape
    e = o.shape[-1]
    gate_blk = z.shape[1] // e - 1
    return pl.pallas_call(
        _s_c_out_kernel,
        grid=(1,),
        in_specs=[
            pl.BlockSpec((bd, e), lambda i: (0, 0)),
            pl.BlockSpec((bd, e), lambda i: (0, gate_blk)),
            pl.BlockSpec((bd, d), lambda i: (0, 0)),
            pl.BlockSpec(wout.shape, lambda i: (0, 0)),
        ],
        out_specs=pl.BlockSpec((bd, d), lambda i: (0, 0)),
        out_shape=jax.ShapeDtypeStruct((bd, d), F32),
        compiler_params=_cparams(1),
        name="sample_c_out",
    )(o, z, h, wout)


def _rope_tables(pos):
    half = ROT_DIM // 2
    inv = ROPE_THETA ** (-jnp.arange(half, dtype=F32) * (2.0 / ROT_DIM))
    ang = pos.astype(F32)[:, None] * inv[None, :]
    cos, sin = jnp.cos(ang), jnp.sin(ang)
    n = pos.shape[0]
    rest = HEAD_DIM - ROT_DIM
    cosf = jnp.concatenate([cos, cos, jnp.ones((n, rest), F32)], axis=1)
    s1 = jnp.concatenate([-sin, jnp.zeros((n, half + rest), F32)], axis=1)
    s2 = jnp.concatenate([jnp.zeros((n, half), F32), sin, jnp.zeros((n, rest), F32)], axis=1)
    return cosf, s1, s2


def _stream_major(x, dil):
    if dil == 1:
        return x
    rows = x.shape[0]
    return jnp.swapaxes(x.reshape((rows // dil, dil) + x.shape[1:]), 0, 1).reshape(x.shape)


def kernel(x_prompt, x_sample, state_conv, cache_kv_w128, cache_kv_w512, cache_kv_w2048,
           norm_w, final_norm_w, a_w_in, a_ln_g, a_ln_b, a_w_s, a_b_s, a_w_out,
           b_w_in, b_conv_w, b_w_out, c_w_in, c_w_out):
    caches = (cache_kv_w128, cache_kv_w512, cache_kv_w2048)
    batch, seq, d = x_prompt.shape
    bd, t_dec, _ = x_sample.shape
    assert t_dec == 1, "sample kernels handle one new token per sequence"
    depth = norm_w.shape[0]
    e = a_ln_g.shape[-1]
    ge = e // A_GROUPS
    heads = e // HEAD_DIM
    m = batch * seq

    hp = x_prompt.reshape(m, d)
    hs = x_sample.reshape(bd, d)
    fnw = final_norm_w.reshape(1, d)
    tabs_tok = _rope_tables(jnp.arange(seq))
    tabs_g = [tuple(_stream_major(t, dil) for t in tabs_tok) for _, dil in DIL_GROUPS]
    tabs_s = _rope_tables(PAST_LEN + jnp.arange(t_dec))

    v_rows, conv_p, conv_s = [], [], []
    kv_p = [[] for _ in DIL_GROUPS]
    kv_s = [[] for _ in DIL_GROUPS]
    for i in range(depth):
        kind, j = i % N_MIXERS, i // N_MIXERS
        nw = norm_w[i].reshape(1, d)
        last = i == depth - 1
        if kind == 0:
            win = a_w_in[j].astype(BF16)
            wout = a_w_out[j].astype(BF16)
            lng = a_ln_g[j].reshape(1, e)
            lnb = a_ln_b[j].reshape(1, e)
            ws = a_w_s[j].astype(BF16)
            bias = jnp.repeat(a_b_s[j].T, ge, axis=1)
            hp = _layer_a(hp, nw, win, lng, lnb, ws, bias, wout, fnw, final_norm=last)
            z = _s_proj(hs, nw, win, tn=e)
            w0 = jnp.repeat(a_w_s[j][:, 0, 0], ge).reshape(1, e)
            b0 = jnp.repeat(a_b_s[j][:, 0], ge).reshape(1, e)
            hs, v_new = _s_a(z, lng, lnb, w0, b0, hs, wout, fnw, final_norm=last)
            v_rows.append(v_new.reshape(bd, t_dec, e))
        elif kind == 1:
            win = b_w_in[j].astype(BF16)
            wout = b_w_out[j].astype(BF16)
            cw = b_conv_w[j]
            hp, st = _layer_b(hp, nw, win, cw, wout, batch=batch)
            conv_p.append(st)
            z = _s_proj(hs, nw, win, tn=e)
            st0, st1 = state_conv[j][:, 0], state_conv[j][:, 1]
            hs, u_new = _s_b(z, st0, st1, cw, hs, wout)
            conv_s.append(jnp.stack([st1, u_new], axis=1))
        else:
            win = c_w_in[j].astype(BF16)
            wout = c_w_out[j].astype(BF16)
            ns = _c_prep(hp, nw, batch=batch)
            os_, lses, qkv0 = [], [], None
            for gi, (_, dil) in enumerate(DIL_GROUPS):
                qkv = _c_proj(ns[gi], win, tabs_g[gi], gi, e=e)
                qkv0 = qkv if gi == 0 else qkv0
                o, lse = _attn_group(qkv, gi, batch=batch)
                os_.append(o)
                lses.append(jnp.swapaxes(lse, 1, 2).reshape(m, heads))
                tail = _c_tail(ns[0], win, tabs_tok, gi, batch=batch, e=e)
                kv_p[gi].append(tail.reshape(batch, tail.shape[1], 2, heads, HEAD_DIM))
            hp = _c_finish(os_, lses, qkv0, hp, wout, batch=batch)
            z = _s_proj(hs, nw, win, tn=e)
            o_s, kv_new = _s_attn(z, *tabs_s, [c[j] for c in caches], e=e)
            hs = _s_c_out(o_s, z, hs, wout)
            for gi, t in enumerate(kv_new):
                kv_s[gi].append(t.reshape(bd, t_dec, 2, heads, HEAD_DIM))
        if last and kind != 0:
            raise NotImplementedError("final norm is fused into the gMLP layer kernels")

    y_prompt = hp.reshape(batch, seq, d)
    y_sample = hs.reshape(bd, t_dec, d)
    kvp = [jnp.stack(l) for l in kv_p]
    kvs = [jnp.stack(l) for l in kv_s]
    return (y_prompt, y_sample, jnp.stack(v_rows), jnp.stack(conv_p), jnp.stack(conv_s),
            kvp[0], kvs[0], kvp[1], kvs[1], kvp[2], kvs[2])
```
